```python
import jax, jax.numpy as jnp
from jax import lax
import numpy as np

D_MODEL = 2048
BATCH = 2
SEQ = 4096
DEPTH = 2
DEC_BATCH = 8
DEC_SEQ = 8
PAST_LEN = 16384
PAGE_SIZE = 128

D_A = D_MODEL // 2
DH_A = 128
H_A = D_A // DH_A
D_B = D_MODEL - D_A
DH_B = 64
H_B = D_B // DH_B
LORA_W = 96
LORA_A = 96
LORA_G = 256
B_COLS = 3 * D_B + LORA_W + LORA_A + LORA_G
SPLIT_B = [D_B, 2 * D_B, 3 * D_B, 3 * D_B + LORA_W, 3 * D_B + LORA_W + LORA_A]
IN_COLS = 3 * D_A + B_COLS
POOL_WINDOWS = (2, 4, 8, 16)
N_POOL_GROUPS = len(POOL_WINDOWS)
G_POOL = D_MODEL // N_POOL_GROUPS
POOL_BUF = max(POOL_WINDOWS) - 1
D_FF = 4 * D_MODEL
N_AB = (DEPTH + 1) // 2
N_C = DEPTH // 2
Q_BLOCK = 128
EPS_RMS = 1e-6
EPS_LNX = 64e-5

kernel_name = 'hybrid_stickbreak_rwkv7_pool_decoder_step'


def rms_norm(x, g):
    xf = x.astype(jnp.float32)
    y = xf * lax.rsqrt(jnp.mean(xf * xf, axis=-1, keepdims=True) + EPS_RMS)
    return (y * g.astype(jnp.float32)).astype(x.dtype)


def adaln(c, w_ada, b_ada):
    mod = jax.nn.silu(c) @ w_ada + b_ada
    return jnp.split(mod[:, None, :], 6, axis=-1)


def modulate(x, g, shift, scale):
    return rms_norm(x, g) * (1 + scale) + shift


def stick_breaking(q, q_pos, k, v, k_pos, bias):
    z = (jnp.einsum('bqhd,bkhd->bhqk', q, k).astype(jnp.float32) * (DH_A ** -0.5)
         + bias.astype(jnp.float32)[None, :, None, None])
    mask = k_pos[None, :] < q_pos[:, None]
    neg = jnp.where(mask, jax.nn.log_sigmoid(-z), 0.0)
    after = lax.cumsum(neg, axis=3, reverse=True) - neg
    w = jnp.where(mask, jnp.exp(jax.nn.log_sigmoid(z) + after), 0.0)
    return jnp.einsum('bhqk,bkhd->bqhd', w.astype(v.dtype), v)


def stick_breaking_prompt(q, k, v, bias):
    Bt, S = q.shape[:2]
    nb = S // Q_BLOCK
    pos = jnp.arange(S, dtype=jnp.int32)
    qb = q.reshape(Bt, nb, Q_BLOCK, H_A, DH_A).transpose(1, 0, 2, 3, 4)
    qpos = pos.reshape(nb, Q_BLOCK)
    out = lax.map(lambda blk: stick_breaking(blk[0], blk[1], k, v, pos, bias), (qb, qpos))
    return out.transpose(1, 0, 2, 3, 4).reshape(Bt, S, H_A, DH_A)


def make_attend(past, bias):
    if past is None:
        return lambda q, k, v: stick_breaking_prompt(q, k, v, bias)
    k_past, v_past = past

    def attend(q, k, v):
        n_past, T = k_past.shape[1], q.shape[1]
        kf = jnp.concatenate([k_past, k], axis=1)
        vf = jnp.concatenate([v_past, v], axis=1)
        q_pos = n_past + jnp.arange(T, dtype=jnp.int32)
        k_pos = jnp.arange(n_past + T, dtype=jnp.int32)
        return stick_breaking(q, q_pos, kf, vf, k_pos, bias)
    return attend


def rwkv7_scan(r, decay, k, v, kk, a, s0):
    def step(S, inp):
        r_t, w_t, k_t, v_t, kk_t, a_t = inp
        sa = jnp.einsum('bhvk,bhk->bhv', S, -kk_t)
        S = (S * w_t[:, :, None, :] + sa[..., None] * (kk_t * a_t)[:, :, None, :]
             + v_t[..., None] * k_t[:, :, None, :])
        return S, jnp.einsum('bhvk,bhk->bhv', S, r_t)
    xs = tuple(jnp.swapaxes(t, 0, 1) for t in (r, decay, k, v, kk, a))
    S, y = lax.scan(step, s0, xs)
    return jnp.swapaxes(y, 0, 1), S


def rwkv7_branch(pb, pb_prev, s0, W, i):
    Bt, T, _ = pb.shape
    f32 = jnp.float32
    heads = lambda t: t.reshape(Bt, T, H_B, DH_B)
    z = pb + W['mu_shift'][i] * (jnp.concatenate([pb_prev, pb[:, :-1]], axis=1) - pb)
    zr, zk, zv, zw, za, zg = jnp.split(z, SPLIT_B, axis=-1)
    w_log = -jax.nn.softplus(-(W['w0'][i] + jnp.tanh(zw) @ W['w_up'][i])) - 0.5
    decay = jnp.exp(-jnp.exp(w_log.astype(f32)))
    a = jax.nn.sigmoid(W['a0'][i] + za @ W['a_up'][i])
    g = jax.nn.sigmoid(zg) @ W['g_up'][i]
    kk = heads((zk * W['k_k'][i]).astype(f32))
    kk = kk * lax.rsqrt(jnp.sum(kk * kk, axis=-1, keepdims=True) + 1e-12)
    k = zk * (1 + (a - 1) * W['k_a'][i])
    r4, k4, v4, a4 = [heads(t.astype(f32)) for t in (zr, k, zv, a)]
    y, s_new = rwkv7_scan(r4, heads(decay), k4, v4, kk, a4, s0.astype(f32))
    mean = jnp.mean(y, axis=-1, keepdims=True)
    var = jnp.mean(jnp.square(y - mean), axis=-1, keepdims=True)
    y = ((y - mean) * lax.rsqrt(var + EPS_LNX)).reshape(Bt, T, D_B) * W['lnx_g'][i] + W['lnx_b'][i]
    bonus = jnp.sum(r4 * k4 * W['r_k'][i], axis=-1, keepdims=True) * v4
    y = y + bonus.reshape(Bt, T, D_B)
    return (y * g).astype(pb.dtype), s_new


def ab_mixer(h, attend, shift_prev, s0, W, i):
    Bt, T, _ = h.shape
    p = h @ W['w_in'][i]
    q = p[..., :D_A].reshape(Bt, T, H_A, DH_A)
    k = p[..., D_A:2 * D_A].reshape(Bt, T, H_A, DH_A)
    v = p[..., 2 * D_A:3 * D_A].reshape(Bt, T, H_A, DH_A)
    o_a = attend(q, k, v).reshape(Bt, T, D_A)
    pb = p[..., 3 * D_A:]
    o_b, s_new = rwkv7_branch(pb, shift_prev[:, None, :], s0, W, i)
    out = jnp.concatenate([o_a, o_b.astype(o_a.dtype)], axis=-1) @ W['w_out'][i]
    return out, k, v, pb[:, -1], s_new


def pool_mixer(ext, n_prev, pos0, w_pool, pool_scale):
    Bt, L, _ = ext.shape
    T = L - n_prev
    xf = ext.astype(jnp.float32)
    csum = jnp.concatenate([jnp.zeros_like(xf[:, :1]), jnp.cumsum(xf, axis=1)], axis=1)
    t_ext = jnp.arange(n_prev, L, dtype=jnp.int32)
    pos = pos0 + jnp.arange(T, dtype=jnp.int32)
    cur = xf[:, n_prev:]
    groups = []
    for gi, wlen in enumerate(POOL_WINDOWS):
        sl = slice(gi * G_POOL, (gi + 1) * G_POOL)
        lo = jnp.maximum(t_ext + 1 - wlen, 0)
        win = csum[:, t_ext + 1, sl] - csum[:, lo, sl]
        cnt = jnp.minimum(wlen, pos + 1).astype(jnp.float32)
        groups.append(win / cnt[None, :, None] - cur[..., sl])
    m = jnp.stack(groups, axis=2)
    y = jnp.einsum('btgc,gcd->btgd', m, w_pool.astype(jnp.float32)).reshape(Bt, T, D_MODEL)
    return (y * pool_scale).astype(ext.dtype)


def mlp(h, w1, w2):
    return jnp.square(jax.nn.relu(h @ w1)) @ w2


def trunk(x, c, pos0, past_kv, shift0, wkv0, pool0, W):
    ks, vs, wkvs, shifts, pools = [], [], [], [], []
    for l in range(DEPTH):
        mod = adaln(c, W['w_ada'][l], W['b_ada'][l])
        g = W['norm_g'][l]
        h = modulate(x, g[0], mod[0], mod[1])
        if l % 2 == 0:
            i = l // 2
            attend = make_attend(None if past_kv is None else past_kv[i], W['sb_bias'][i])
            out, k_new, v_new, sh, s_new = ab_mixer(h, attend, shift0[i], wkv0[i], W, i)
            ks.append(k_new)
            vs.append(v_new)
            shifts.append(sh)
            wkvs.append(s_new)
        else:
            j = l // 2
            ext = jnp.concatenate([pool0[j].astype(h.dtype), h], axis=1)
            out = pool_mixer(ext, pool0[j].shape[1], pos0, W['w_pool'][j], W['pool_scale'][j])
            pools.append(ext[:, -POOL_BUF:])
        x = x + mod[2] * rms_norm(out, g[1])
        h = modulate(x, g[2], mod[3], mod[4])
        x = x + mod[5] * rms_norm(mlp(h, W['w_mlp1'][l], W['w_mlp2'][l]), g[3])
    return (x, jnp.stack(ks, axis=1), jnp.stack(vs, axis=1), jnp.stack(wkvs, axis=0),
            jnp.stack(shifts, axis=0), jnp.stack(pools, axis=0))


def setup_inputs(seed: int = 0) -> dict:
    key = jax.random.key(seed)
    keys = iter(jax.random.split(key, 48))
    f32 = jnp.float32
    nrm = lambda shape, s: jax.random.normal(next(keys), shape, f32) * s
    n_pages = PAST_LEN // PAGE_SIZE
    n_used = DEC_BATCH * n_pages
    n_pool = n_used + (n_used + 3) // 4
    perm = jax.random.permutation(next(keys), n_pool)
    page_table = perm[:n_used].reshape(DEC_BATCH, n_pages).astype(jnp.int32)
    return {
        'x_prompt': nrm((BATCH, SEQ, D_MODEL), 1.0),
        'x_sample': nrm((DEC_BATCH, DEC_SEQ, D_MODEL), 1.0),
        'cache_k': nrm((n_pool, N_AB, PAGE_SIZE, H_A, DH_A), 1.0),
        'cache_v': nrm((n_pool, N_AB, PAGE_SIZE, H_A, DH_A), 1.0),
        'page_table': page_table,
        'state_wkv': nrm((N_AB, DEC_BATCH, H_B, DH_B, DH_B), 1.0),
        'state_shift': nrm((N_AB, DEC_BATCH, B_COLS), 1.0),
        'state_pool': nrm((N_C, DEC_BATCH, POOL_BUF, D_MODEL), 1.0),
        'c_prompt': nrm((BATCH, D_MODEL), 1.0),
        'c_sample': nrm((DEC_BATCH, D_MODEL), 1.0),
        'w_ada': nrm((DEPTH, D_MODEL, 6 * D_MODEL), 0.5 * D_MODEL ** -0.5),
        'b_ada': nrm((DEPTH, 6 * D_MODEL), 0.01),
        'norm_g': 1.0 + nrm((DEPTH, 4, D_MODEL), 0.05),
        'w_in': nrm((N_AB, D_MODEL, IN_COLS), D_MODEL ** -0.5),
        'w_out': nrm((N_AB, D_MODEL, D_MODEL), D_MODEL ** -0.5),
        'sb_bias': -6.0 + nrm((N_AB, H_A), 0.5),
        'mu_shift': jax.random.uniform(next(keys), (N_AB, B_COLS), f32),
        'w0': -1.0 + nrm((N_AB, D_B), 0.5),
        'w_up': nrm((N_AB, LORA_W, D_B), 0.1 * LORA_W ** -0.5),
        'a0': nrm((N_AB, D_B), 0.1),
        'a_up': nrm((N_AB, LORA_A, D_B), 0.1 * LORA_A ** -0.5),
        'g_up': nrm((N_AB, LORA_G, D_B), LORA_G ** -0.5),
        'k_k': 0.85 + nrm((N_AB, D_B), 0.05),
        'k_a': 1.0 + nrm((N_AB, D_B), 0.05),
        'r_k': nrm((N_AB, H_B, DH_B), 0.1),
        'lnx_g': 1.0 + nrm((N_AB, D_B), 0.05),
        'lnx_b': nrm((N_AB, D_B), 0.01),
        'w_pool': nrm((N_C, N_POOL_GROUPS, G_POOL, G_POOL), G_POOL ** -0.5),
        'pool_scale': 1.0 + nrm((N_C, D_MODEL), 0.1),
        'w_mlp1': nrm((DEPTH, D_MODEL, D_FF), D_MODEL ** -0.5),
        'w_mlp2': nrm((DEPTH, D_FF, D_MODEL), D_FF ** -0.5),
    }


def reference(x_prompt, x_sample, cache_k, cache_v, page_table, state_wkv, state_shift, state_pool,
              c_prompt, c_sample, w_ada, b_ada, norm_g, w_in, w_out, sb_bias, mu_shift, w0, w_up, a0,
              a_up, g_up, k_k, k_a, r_k, lnx_g, lnx_b, w_pool, pool_scale, w_mlp1, w_mlp2):
    W = {'w_ada': w_ada, 'b_ada': b_ada, 'norm_g': norm_g, 'w_in': w_in, 'w_out': w_out,
         'sb_bias': sb_bias, 'mu_shift': mu_shift, 'w0': w0, 'w_up': w_up, 'a0': a0, 'a_up': a_up,
         'g_up': g_up, 'k_k': k_k, 'k_a': k_a, 'r_k': r_k, 'lnx_g': lnx_g, 'lnx_b': lnx_b,
         'w_pool': w_pool, 'pool_scale': pool_scale, 'w_mlp1': w_mlp1, 'w_mlp2': w_mlp2}
    bp, bs = x_prompt.shape[0], x_sample.shape[0]
    shift0 = jnp.zeros((N_AB, bp, B_COLS), x_prompt.dtype)
    wkv0 = jnp.zeros((N_AB, bp, H_B, DH_B, DH_B), jnp.float32)
    pool0 = jnp.zeros((N_C, bp, 0, D_MODEL), x_prompt.dtype)
    y_prompt, k_p, v_p, wkv_p, sh_p, pool_p = trunk(x_prompt, c_prompt, 0, None, shift0, wkv0, pool0, W)
    past_len = page_table.shape[1] * PAGE_SIZE
    past_kv = [(cache_k[page_table, i].reshape(bs, past_len, H_A, DH_A),
                cache_v[page_table, i].reshape(bs, past_len, H_A, DH_A)) for i in range(N_AB)]
    y_sample, k_s, v_s, wkv_s, sh_s, pool_s = trunk(x_sample, c_sample, past_len, past_kv,
                                                    state_shift, state_wkv, state_pool, W)
    return (y_prompt, y_sample, k_p, v_p, k_s, v_s, wkv_p, wkv_s, sh_p, sh_s, pool_p, pool_s)
```

```python
import functools

import jax
import jax.numpy as jnp
import numpy as np
from jax import lax
from jax.experimental import pallas as pl
from jax.experimental.pallas import tpu as pltpu

F32 = jnp.float32
BF16 = jnp.bfloat16

EPS_RMS = 1e-6
EPS_LNX = 64e-5
EPS_KK = 1e-12
POOL_WINDOWS = (2, 4, 8, 16)
POOL_HALO = 16
POOL_TILE = 256
DH_A = 128
DH_B = 64
LORA_PAD = (128, 128, 256)
CHUNK = 64
PAGES_PER_STEP = 4
VMEM_LIMIT = 56 * 1024 * 1024


def _cparams(sem):
    return pltpu.CompilerParams(dimension_semantics=sem, vmem_limit_bytes=VMEM_LIMIT)


def _dot(a, b):
    return jnp.dot(a, b, preferred_element_type=F32)


def _dot_nt(a, b):
    return lax.dot_general(a, b, (((1,), (1,)), ((), ())), preferred_element_type=F32)


def _dot_tn(a, b):
    return lax.dot_general(a, b, (((0,), (0,)), ((), ())), preferred_element_type=F32)


def _split2(x):
    hi = x.astype(BF16)
    lo = (x - hi.astype(F32)).astype(BF16)
    return hi, lo


def _split3(x):
    hi = x.astype(BF16)
    r1 = x - hi.astype(F32)
    mid = r1.astype(BF16)
    lo = (r1 - mid.astype(F32)).astype(BF16)
    return hi, mid, lo


def _dot_exact_rhs(x, m, parts):
    pieces = _split2(x) if parts == 2 else _split3(x)
    out = _dot(pieces[0], m)
    for p in pieces[1:]:
        out = out + _dot(p, m)
    return out


def _sigmoid(x):
    return 1.0 / (1.0 + jnp.exp(-x))


def _softplus(x):
    return jnp.maximum(x, 0.0) + jnp.log1p(jnp.exp(-jnp.abs(x)))


def _rms(x, g):
    ms = jnp.mean(x * x, axis=-1, keepdims=True)
    return x * lax.rsqrt(ms + EPS_RMS) * g


def _adaln_kernel(c_ref, w_ref, b_ref, o_ref):
    c = c_ref[...]
    s = c * _sigmoid(c)
    w = w_ref[0]
    s_hi, s_lo = _split2(s)
    w_hi, w_lo = _split2(w)
    o_ref[0] = _dot(s_hi, w_hi) + _dot(s_lo, w_hi) + _dot(s_hi, w_lo) + b_ref[0]


def _adaln(c_all, w_ada, b_ada):
    n_layers, d, n = w_ada.shape
    nb = c_all.shape[0]
    tn = 512
    return pl.pallas_call(
        _adaln_kernel,
        out_shape=jax.ShapeDtypeStruct((n_layers, nb, n), F32),
        grid=(n_layers, n // tn),
        in_specs=[
            pl.BlockSpec((nb, d), lambda l, j: (0, 0)),
            pl.BlockSpec((1, d, tn), lambda l, j: (l, 0, j)),
            pl.BlockSpec((1, 1, tn), lambda l, j: (l, 0, j)),
        ],
        out_specs=pl.BlockSpec((1, nb, tn), lambda l, j: (l, 0, j)),
        compiler_params=_cparams(("arbitrary", "arbitrary")),
        name="adaln",
    )(c_all, w_ada, b_ada.reshape(n_layers, 1, n))


def _row_tiling(nb, t, tt_max):
    if t >= 64:
        return 1, min(t, tt_max)
    return nb, t


def _modulate(x, g, shift, scale):
    return _rms(x, g) * (1.0 + scale) + shift


def _proj_kernel(x_ref, g_ref, sh_ref, sc_ref, w_ref, o_ref, h_scr):
    @pl.when(pl.program_id(1) == 0)
    def _():
        h = _modulate(x_ref[...], g_ref[...], sh_ref[...], sc_ref[...])
        h_scr[...] = h.reshape(h_scr.shape).astype(BF16)

    o_ref[...] = _dot(h_scr[...], w_ref[...])


def _proj_in(x, g, shift, scale, w_bf16):
    nb, t, d = x.shape
    n = w_bf16.shape[1]
    nbk, tt = _row_tiling(nb, t, 512)
    n_t = t // tt
    rows = nbk * tt
    tn = 512
    return pl.pallas_call(
        _proj_kernel,
        out_shape=jax.ShapeDtypeStruct((nb * t, n), F32),
        grid=((nb // nbk) * n_t, n // tn),
        in_specs=[
            pl.BlockSpec((nbk, tt, d), lambda i, j: (i // n_t, i % n_t, 0)),
            pl.BlockSpec((1, d), lambda i, j: (0, 0)),
            pl.BlockSpec((nbk, 1, d), lambda i, j: (i // n_t, 0, 0)),
            pl.BlockSpec((nbk, 1, d), lambda i, j: (i // n_t, 0, 0)),
            pl.BlockSpec((d, tn), lambda i, j: (0, j)),
        ],
        out_specs=pl.BlockSpec((rows, tn), lambda i, j: (i, j)),
        scratch_shapes=[pltpu.VMEM((rows, d), BF16)],
        compiler_params=_cparams(("arbitrary", "arbitrary")),
        name="proj_in",
    )(x, g.reshape(1, d), shift, scale, w_bf16)


def _suffix_ones(n):
    j = lax.broadcasted_iota(jnp.int32, (n, n), 0)
    s = lax.broadcasted_iota(jnp.int32, (n, n), 1)
    m = jnp.where(j > s, 1.0, 0.0).astype(BF16)
    return jnp.concatenate([m, m], axis=0)


def _sb_tile(z, carry, mask, uu):
    sp = _softplus(z)
    neg = -sp
    if mask is not None:
        neg = jnp.where(mask, neg, 0.0)
    hi, lo = _split2(neg)
    local = _dot(jnp.concatenate([hi, lo], axis=1), uu)
    reps = z.shape[1] // carry.shape[1]
    after = local + (carry if reps == 1 else jnp.concatenate([carry] * reps, axis=1))
    w = jnp.exp(z - sp + after)
    if mask is not None:
        w = jnp.where(mask, w, 0.0)
    carry = carry + jnp.sum(neg, axis=-1, keepdims=True)
    return w, carry


def _attn_prompt_kernel(qi_ref, kb_ref, q_ref, k_ref, v_ref, bias_ref, o_ref,
                        acc_scr, carry_scr, *, tq, tk, sub, n_heads, scale):
    s = pl.program_id(1)
    qi = qi_ref[s]
    kb = kb_ref[s]

    @pl.when(kb == qi)
    def _():
        acc_scr[...] = jnp.zeros_like(acc_scr)
        carry_scr[...] = jnp.zeros_like(carry_scr)

    uu = _suffix_ones(sub)
    q_pos = qi * tq + lax.broadcasted_iota(jnp.int32, (tq, sub), 0)
    k_lane = lax.broadcasted_iota(jnp.int32, (tq, sub), 1)
    for h in range(n_heads):
        cols = slice(h * DH_A, (h + 1) * DH_A)
        q = (q_ref[:, cols] * scale).astype(BF16)
        bias = bias_ref[h:h + 1, :sub]
        carry = carry_scr[h]
        acc = acc_scr[:, cols]
        for sb in reversed(range(tk // sub)):
            rows = slice(sb * sub, (sb + 1) * sub)
            k = k_ref[rows, cols].astype(BF16)
            v = v_ref[rows, cols].astype(BF16)
            z = _dot_nt(q, k) + bias
            mask = (kb * tk + sb * sub + k_lane) < q_pos
            w, carry = _sb_tile(z, carry, mask, uu)
            acc = acc + _dot(w.astype(BF16), v)
        carry_scr[h] = carry
        acc_scr[:, cols] = acc

    @pl.when(kb == 0)
    def _():
        o_ref[...] = acc_scr[...].astype(o_ref.dtype)


def _attn_prompt(p, bias_rows, nb, t, d_a):
    n_heads = d_a // DH_A
    tq = tk = min(512, t)
    sub = min(256, tk)
    nq = t // tq
    pairs = [(qi, kb) for qi in range(nq) for kb in range(qi, -1, -1)]
    qi_tab = jnp.asarray(np.array([a for a, _ in pairs], np.int32))
    kb_tab = jnp.asarray(np.array([b for _, b in pairs], np.int32))
    kern = functools.partial(_attn_prompt_kernel, tq=tq, tk=tk, sub=sub, n_heads=n_heads,
                             scale=DH_A ** -0.5)
    grid_spec = pltpu.PrefetchScalarGridSpec(
        num_scalar_prefetch=2,
        grid=(nb, len(pairs)),
        in_specs=[
            pl.BlockSpec((tq, d_a), lambda b, s, qt, kt: (b * nq + qt[s], 0)),
            pl.BlockSpec((tk, d_a), lambda b, s, qt, kt: (b * nq + kt[s], 1)),
            pl.BlockSpec((tk, d_a), lambda b, s, qt, kt: (b * nq + kt[s], 2)),
            pl.BlockSpec(bias_rows.shape, lambda b, s, qt, kt: (0, 0)),
        ],
        out_specs=pl.BlockSpec((tq, d_a), lambda b, s, qt, kt: (b * nq + qt[s], 0)),
        scratch_shapes=[pltpu.VMEM((tq, d_a), F32), pltpu.VMEM((n_heads, tq, 128), F32)],
    )
    return pl.pallas_call(
        kern,
        out_shape=jax.ShapeDtypeStruct((nb * t, d_a), BF16),
        grid_spec=grid_spec,
        compiler_params=_cparams(("arbitrary", "arbitrary")),
        name="attn_prompt",
    )(qi_tab, kb_tab, p, p, p, bias_rows)


def _attn_sample_kernel(pt_ref, q_ref, kc_ref, vc_ref, bias_ref, *rest,
                        n_heads, t, page, scale):
    n_pg = PAGES_PER_STEP
    k_refs = rest[:n_pg]
    v_refs = rest[n_pg:2 * n_pg]
    o_ref = rest[2 * n_pg]
    acc_scr, carry_scr, kpad_scr, vpad_scr = rest[2 * n_pg + 1:]
    s = pl.program_id(1)
    rows = n_heads * t
    uu = _suffix_ones(page)

    def q_heads():
        return [(q_ref[:, h * DH_A:(h + 1) * DH_A] * scale).astype(BF16) for h in range(n_heads)]

    def block(k_at, v_at, mask):
        qs = q_heads()
        z = jnp.concatenate([_dot_nt(qs[h], k_at(h).astype(BF16)) for h in range(n_heads)],
                            axis=0) + bias_ref[...]
        w, carry = _sb_tile(z, carry_scr[...], mask, uu)
        carry_scr[...] = carry
        w = w.astype(BF16)
        upd = jnp.concatenate(
            [_dot(w[h * t:(h + 1) * t], v_at(h).astype(BF16)) for h in range(n_heads)], axis=0)
        acc_scr[...] += upd

    @pl.when(s == 0)
    def _():
        acc_scr[...] = jnp.zeros_like(acc_scr)
        carry_scr[...] = jnp.zeros_like(carry_scr)
        kpad_scr[...] = jnp.zeros_like(kpad_scr)
        vpad_scr[...] = jnp.zeros_like(vpad_scr)
        kpad_scr[0:t, :] = kc_ref[...]
        vpad_scr[0:t, :] = vc_ref[...]
        q_tok = lax.broadcasted_iota(jnp.int32, (rows, page), 0) % t
        key = lax.broadcasted_iota(jnp.int32, (rows, page), 1)
        block(lambda h: kpad_scr[:, h * DH_A:(h + 1) * DH_A],
              lambda h: vpad_scr[:, h * DH_A:(h + 1) * DH_A], key < q_tok)

    @pl.when(s > 0)
    def _():
        for j in range(n_pg):
            block(lambda h, j=j: k_refs[j][0, 0, :, h * DH_A:(h + 1) * DH_A],
                  lambda h, j=j: v_refs[j][0, 0, :, h * DH_A:(h + 1) * DH_A], None)

    @pl.when(s == pl.num_programs(1) - 1)
    def _():
        for h in range(n_heads):
            o_ref[:, h * DH_A:(h + 1) * DH_A] = acc_scr[h * t:(h + 1) * t, :].astype(o_ref.dtype)


def _attn_sample(p, cache_k, cache_v, page_table, layer, bias_rows, nb, t, d_a):
    n_heads = d_a // DH_A
    n_pool, n_ab, page, _, _ = cache_k.shape
    n_pages = page_table.shape[1]
    n_pg = PAGES_PER_STEP
    assert n_pages % n_pg == 0 and page == 128 and t == 8
    ck = cache_k.reshape(n_pool, n_ab, page, d_a)
    cv = cache_v.reshape(n_pool, n_ab, page, d_a)
    n_steps = 1 + n_pages // n_pg

    def page_map(j):
        def index_map(b, s, pt):
            idx = n_pages - 1 - (jnp.maximum(s, 1) - 1) * n_pg - j
            return (pt[b, idx], layer, 0, 0)
        return index_map

    cache_specs = [pl.BlockSpec((1, 1, page, d_a), page_map(j)) for j in range(n_pg)]
    kern = functools.partial(_attn_sample_kernel, n_heads=n_heads, t=t, page=page,
                             scale=DH_A ** -0.5)
    grid_spec = pltpu.PrefetchScalarGridSpec(
        num_scalar_prefetch=1,
        grid=(nb, n_steps),
        in_specs=[
            pl.BlockSpec((t, d_a), lambda b, s, pt: (b, 0)),
            pl.BlockSpec((t, d_a), lambda b, s, pt: (b, 1)),
            pl.BlockSpec((t, d_a), lambda b, s, pt: (b, 2)),
            pl.BlockSpec(bias_rows.shape, lambda b, s, pt: (0, 0)),
        ] + cache_specs + cache_specs,
        out_specs=pl.BlockSpec((t, d_a), lambda b, s, pt: (b, 0)),
        scratch_shapes=[pltpu.VMEM((n_heads * t, DH_A), F32), pltpu.VMEM((n_heads * t, 128), F32),
                        pltpu.VMEM((page, d_a), F32), pltpu.VMEM((page, d_a), F32)],
    )
    return pl.pallas_call(
        kern,
        out_shape=jax.ShapeDtypeStruct((nb * t, d_a), BF16),
        grid_spec=grid_spec,
        compiler_params=_cparams(("arbitrary", "arbitrary")),
        name="attn_sample",
    )(page_table, p, p, p, bias_rows, *([ck] * n_pg), *([cv] * n_pg))


def _head_ones(n, width):
    i = lax.broadcasted_iota(jnp.int32, (n, n), 0) // width
    j = lax.broadcasted_iota(jnp.int32, (n, n), 1) // width
    return jnp.where(i == j, 1.0, 0.0).astype(BF16)


def _rwkv_pre_kernel(zr_ref, zk_ref, zv_ref, zl_ref, sr_ref, sk_ref, sv_ref, sl_ref,
                     mr_ref, mk_ref, mv_ref, ml_ref, w0_ref, wup_ref, a0_ref, aup_ref, gup_ref,
                     kk_ref, ka_ref, rk_ref,
                     r_out, lw_out, k_out, v_out, a_out, b_out, g_out, bonus_out,
                     prev_r, prev_k, prev_v, prev_l):
    first = pl.program_id(1) == 0

    def shifted(x_ref, init_ref, prev_scr, mu_ref):
        @pl.when(first)
        def _():
            prev_scr[...] = init_ref[0]

        x = x_ref[...]
        n = x.shape[0]
        prev = jnp.where(lax.broadcasted_iota(jnp.int32, x.shape, 0) == 0, prev_scr[...],
                         pltpu.roll(x, 1, 0))
        prev_scr[...] = x[n - 1:n, :]
        return x + mu_ref[...] * (prev - x)

    zr = shifted(zr_ref, sr_ref, prev_r, mr_ref)
    zk = shifted(zk_ref, sk_ref, prev_k, mk_ref)
    zv = shifted(zv_ref, sv_ref, prev_v, mv_ref)
    zl = shifted(zl_ref, sl_ref, prev_l, ml_ref)
    pw, pa, pg = LORA_PAD
    zw, za, zg = zl[:, :pw], zl[:, pw:pw + pa], zl[:, pw + pa:pw + pa + pg]

    lw = w0_ref[...] + _dot(jnp.tanh(zw).astype(BF16), wup_ref[...])
    w_log = -_softplus(-lw) - 0.5
    a_gate = _sigmoid(a0_ref[...] + _dot(za.astype(BF16), aup_ref[...]))
    gate = _dot(_sigmoid(zg).astype(BF16), gup_ref[...])

    ones = _head_ones(zk.shape[1], DH_B)
    kk = zk * kk_ref[...]
    kk = kk * lax.rsqrt(_dot_exact_rhs(kk * kk, ones, 2) + EPS_KK)
    k = zk * (1.0 + (a_gate - 1.0) * ka_ref[...])
    bonus = _dot_exact_rhs(zr * k * rk_ref[...], ones, 2) * zv

    r_out[0] = zr
    lw_out[0] = -jnp.exp(w_log)
    k_out[0] = k
    v_out[0] = zv
    a_out[0] = -kk
    b_out[0] = kk * a_gate
    g_out[0] = gate
    bonus_out[0] = bonus


def _rwkv_pre(p, shift0, col0, nb, t, d_b, wts):
    tt = min(t, 256)
    n_t = t // tt
    wl = sum(LORA_PAD)
    assert col0 % d_b == 0 and (col0 + 3 * d_b) % wl == 0
    cb = col0 // d_b
    cl = (col0 + 3 * d_b) // wl
    s_r, s_k, s_v, s_l = shift0
    row = lambda b, i: b * n_t + i
    full = lambda a: pl.BlockSpec(a.shape, lambda b, i: (0,) * a.ndim)
    vec_args = [wts['mu_r'], wts['mu_k'], wts['mu_v'], wts['mu_l'], wts['w0'], wts['w_up'],
                wts['a0'], wts['a_up'], wts['g_up'], wts['k_k'], wts['k_a'], wts['r_k']]
    out = jax.ShapeDtypeStruct((nb, t, d_b), F32)
    out_spec = pl.BlockSpec((1, tt, d_b), lambda b, i: (b, i, 0))
    return pl.pallas_call(
        _rwkv_pre_kernel,
        out_shape=[out] * 8,
        grid=(nb, n_t),
        in_specs=[
            pl.BlockSpec((tt, d_b), lambda b, i: (row(b, i), cb)),
            pl.BlockSpec((tt, d_b), lambda b, i: (row(b, i), cb + 1)),
            pl.BlockSpec((tt, d_b), lambda b, i: (row(b, i), cb + 2)),
            pl.BlockSpec((tt, wl), lambda b, i: (row(b, i), cl)),
            pl.BlockSpec((1, 1, d_b), lambda b, i: (b, 0, 0)),
            pl.BlockSpec((1, 1, d_b), lambda b, i: (b, 0, 0)),
            pl.BlockSpec((1, 1, d_b), lambda b, i: (b, 0, 0)),
            pl.BlockSpec((1, 1, wl), lambda b, i: (b, 0, 0)),
        ] + [full(a) for a in vec_args],
        out_specs=[out_spec] * 8,
        scratch_shapes=[pltpu.VMEM((1, d_b), F32)] * 3 + [pltpu.VMEM((1, wl), F32)],
        compiler_params=_cparams(("arbitrary", "arbitrary")),
        name="rwkv_pre",
    )(p, p, p, p, s_r, s_k, s_v, s_l, *vec_args)


def _rwkv_scan_kernel(r_ref, lw_ref, k_ref, v_ref, a_ref, b_ref, g_ref, bonus_ref,
                      lng_ref, lnb_ref, s0_ref, o_ref, s_out, s_scr, y_scr, *, n_heads):
    c = CHUNK

    @pl.when(pl.program_id(1) == 0)
    def _():
        s_scr[...] = s0_ref[0]

    ti = lax.broadcasted_iota(jnp.int32, (c, c), 0)
    tj = lax.broadcasted_iota(jnp.int32, (c, c), 1)
    incl_f = jnp.where(ti >= tj, 1.0, 0.0).astype(BF16)
    eye = jnp.where(ti == tj, 1.0, 0.0)
    ri = lax.broadcasted_iota(jnp.int32, (2 * c, 2 * c), 0)
    rj = lax.broadcasted_iota(jnp.int32, (2 * c, 2 * c), 1) % c
    tri_mask = jnp.where(ri < c, ri, ri - c + 1) > rj

    lw = lw_ref[0]
    hi, mid, lo = _split3(lw)
    cum = _dot(incl_f, hi) + _dot(incl_f, mid) + _dot(incl_f, lo)
    cum_end = cum[c - 1:c, :]
    r, k, v, a, b = r_ref[0], k_ref[0], v_ref[0], a_ref[0], b_ref[0]
    a_t = a * jnp.exp(cum - lw)
    r_t = r * jnp.exp(cum)
    e_neg = jnp.exp(-cum)
    b_t = b * e_neg
    k_t = k * e_neg
    e_end = jnp.exp(cum_end - cum)
    b_e = b * e_end
    k_e = k * e_end
    g_end = jnp.exp(cum_end)
    zeros_cv = jnp.zeros((c, DH_B), F32)

    for h in range(n_heads):
        cols = slice(h * DH_B, (h + 1) * DH_B)
        s_prev = s_scr[h]
        ar = jnp.concatenate([a_t[:, cols], r_t[:, cols]], axis=0).astype(BF16)
        bk = jnp.concatenate([b_t[:, cols], k_t[:, cols]], axis=0).astype(BF16)
        m = jnp.where(tri_mask, _dot_nt(ar, bk), 0.0)
        ars = _dot_nt(ar, s_prev.astype(BF16))
        v_h = v[:, cols]
        lab = m[:c, :c]
        inv = eye + lab
        pw = lab
        n = 1
        while 2 * n < c:
            pw_b = pw.astype(BF16)
            pw = _dot(pw_b, pw_b)
            inv = inv + _dot(inv.astype(BF16), pw.astype(BF16))
            n *= 2
        m_b = m.astype(BF16)
        rhs = ars[:c] + _dot(m_b[:c], jnp.concatenate([zeros_cv, v_h], axis=0).astype(BF16))
        u = _dot(inv.astype(BF16), rhs.astype(BF16))
        uv = jnp.concatenate([u, v_h], axis=0).astype(BF16)
        y = ars[c:] + _dot(m_b[c:], uv)
        bke = jnp.concatenate([b_e[:, cols], k_e[:, cols]], axis=0).astype(BF16)
        s_scr[h] = s_prev * g_end[:, cols] + _dot_tn(uv, bke)
        mean = jnp.mean(y, axis=-1, keepdims=True)
        yc = y - mean
        var = jnp.mean(yc * yc, axis=-1, keepdims=True)
        y_scr[:, cols] = yc * lax.rsqrt(var + EPS_LNX)

    o_ref[0] = ((y_scr[...] * lng_ref[...] + lnb_ref[...] + bonus_ref[0]) * g_ref[0]).astype(o_ref.dtype)

    @pl.when(pl.program_id(1) == pl.num_programs(1) - 1)
    def _():
        s_out[0] = s_scr[...]


def _rwkv_scan(vecs, lnx_g, lnx_b, s0):
    nb, t, d_b = vecs[0].shape
    n_heads = d_b // DH_B
    tok = pl.BlockSpec((1, CHUNK, d_b), lambda b, i: (b, i, 0))
    vec = pl.BlockSpec((1, d_b), lambda b, i: (0, 0))
    st = pl.BlockSpec((1, n_heads, DH_B, DH_B), lambda b, i: (b, 0, 0, 0))
    return pl.pallas_call(
        functools.partial(_rwkv_scan_kernel, n_heads=n_heads),
        out_shape=[jax.ShapeDtypeStruct((nb, t, d_b), BF16),
                   jax.ShapeDtypeStruct((nb, n_heads, DH_B, DH_B), F32)],
        grid=(nb, t // CHUNK),
        in_specs=[tok] * 8 + [vec, vec, st],
        out_specs=[tok, st],
        scratch_shapes=[pltpu.VMEM((n_heads, DH_B, DH_B), F32), pltpu.VMEM((CHUNK, d_b), F32)],
        compiler_params=_cparams(("arbitrary", "arbitrary")),
        name="rwkv_scan",
    )(*vecs, lnx_g.reshape(1, d_b), lnx_b.reshape(1, d_b), s0)


def _residual(x, gate, out, g):
    return x + gate * _rms(out.reshape(x.shape), g)


def _outproj_kernel(oa_ref, ob_ref, w_ref, x_ref, gate_ref, g_ref, o_ref):
    d_a = oa_ref.shape[1]
    out = _dot(oa_ref[...], w_ref[:d_a, :]) + _dot(ob_ref[...], w_ref[d_a:, :])
    o_ref[...] = _residual(x_ref[...], gate_ref[...], out, g_ref[...])


def _outproj(o_a, o_b, w_bf16, x, gate, g):
    nb, t, d = x.shape
    nbk, tt = _row_tiling(nb, t, 512)
    n_t = t // tt
    rows = nbk * tt
    x_spec = pl.BlockSpec((nbk, tt, d), lambda i: (i // n_t, i % n_t, 0))
    return pl.pallas_call(
        _outproj_kernel,
        out_shape=jax.ShapeDtypeStruct(x.shape, F32),
        grid=((nb // nbk) * n_t,),
        in_specs=[
            pl.BlockSpec((rows, o_a.shape[1]), lambda i: (i, 0)),
            pl.BlockSpec((rows, o_b.shape[1]), lambda i: (i, 0)),
            pl.BlockSpec(w_bf16.shape, lambda i: (0, 0)),
            x_spec,
            pl.BlockSpec((nbk, 1, d), lambda i: (i // n_t, 0, 0)),
            pl.BlockSpec((1, d), lambda i: (0, 0)),
        ],
        out_specs=x_spec,
        compiler_params=_cparams(("arbitrary",)),
        name="outproj",
    )(o_a, o_b, w_bf16, x, gate, g.reshape(1, d))


def _mlp_kernel(x_ref, gin_ref, sh_ref, sc_ref, w1_ref, w2_ref, gate_ref, gout_ref, o_ref,
                h_scr, acc_scr):
    f = pl.program_id(1)

    @pl.when(f == 0)
    def _():
        h = _modulate(x_ref[...], gin_ref[...], sh_ref[...], sc_ref[...])
        h_scr[...] = h.reshape(h_scr.shape).astype(BF16)
        acc_scr[...] = jnp.zeros_like(acc_scr)

    hid = jnp.maximum(_dot(h_scr[...], w1_ref[...]), 0.0)
    acc_scr[...] += _dot((hid * hid).astype(BF16), w2_ref[...])

    @pl.when(f == pl.num_programs(1) - 1)
    def _():
        o_ref[...] = _residual(x_ref[...], gate_ref[...], acc_scr[...], gout_ref[...])


def _mlp(x, g_in, shift, scale, w1_bf16, w2_bf16, gate, g_out):
    nb, t, d = x.shape
    d_ff = w1_bf16.shape[1]
    nbk, tt = _row_tiling(nb, t, 512)
    n_t = t // tt
    rows = nbk * tt
    tf = 1024
    x_spec = pl.BlockSpec((nbk, tt, d), lambda i, f: (i // n_t, i % n_t, 0))
    mod_spec = pl.BlockSpec((nbk, 1, d), lambda i, f: (i // n_t, 0, 0))
    vec_spec = pl.BlockSpec((1, d), lambda i, f: (0, 0))
    return pl.pallas_call(
        _mlp_kernel,
        out_shape=jax.ShapeDtypeStruct(x.shape, F32),
        grid=((nb // nbk) * n_t, d_ff // tf),
        in_specs=[x_spec, vec_spec, mod_spec, mod_spec,
                  pl.BlockSpec((d, tf), lambda i, f: (0, f)),
                  pl.BlockSpec((tf, d), lambda i, f: (f, 0)),
                  mod_spec, vec_spec],
        out_specs=x_spec,
        scratch_shapes=[pltpu.VMEM((rows, d), BF16), pltpu.VMEM((rows, d), F32)],
        compiler_params=_cparams(("arbitrary", "arbitrary")),
        name="mlp",
    )(x, g_in.reshape(1, d), shift, scale, w1_bf16, w2_bf16, gate, g_out.reshape(1, d))


def _modulate_kernel(x_ref, g_ref, sh_ref, sc_ref, o_ref):
    o_ref[...] = _modulate(x_ref[...], g_ref[...], sh_ref[...], sc_ref[...])


def _modulate_call(x, g, shift, scale):
    nb, t, d = x.shape
    nbk, tt = _row_tiling(nb, t, 512)
    n_t = t // tt
    x_spec = pl.BlockSpec((nbk, tt, d), lambda i: (i // n_t, i % n_t, 0))
    mod_spec = pl.BlockSpec((nbk, 1, d), lambda i: (i // n_t, 0, 0))
    return pl.pallas_call(
        _modulate_kernel,
        out_shape=jax.ShapeDtypeStruct(x.shape, F32),
        grid=((nb // nbk) * n_t,),
        in_specs=[x_spec, pl.BlockSpec((1, d), lambda i: (0, 0)), mod_spec, mod_spec],
        out_specs=x_spec,
        compiler_params=_cparams(("arbitrary",)),
        name="modulate",
    )(x, g.reshape(1, d), shift, scale)


def _pool_kernel(h_ref, halo_ref, w_ref, ps_ref, x_ref, gate_ref, g_ref, o_ref,
                 *, pos0, zero_first_halo):
    i = pl.program_id(1)
    cur = h_ref[0]
    tt, d = cur.shape
    gw = d // len(POOL_WINDOWS)
    halo = halo_ref[0]
    if zero_first_halo:
        halo = jnp.where(i == 0, 0.0, halo)
    ext = jnp.concatenate([halo, cur], axis=0)
    pos = pos0 + i * tt + lax.broadcasted_iota(jnp.int32, (tt, 1), 0)
    outs = []
    run = ext
    span = 1
    for gi, wlen in enumerate(POOL_WINDOWS):
        while span < wlen:
            run = run + pltpu.roll(run, span, 0)
            span *= 2
        win = run[POOL_HALO:, :gw]
        cnt = jnp.minimum(wlen, pos + 1).astype(F32)
        m = win / cnt - cur[:, gi * gw:(gi + 1) * gw]
        outs.append(_dot(m.astype(BF16), w_ref[gi]))
        run = run[:, gw:]
    y = jnp.concatenate(outs, axis=1) * ps_ref[...]
    o_ref[...] = _residual(x_ref[...], gate_ref[...], y, g_ref[...])


def _pool(h, halo_src, halo_blocks_per_tile, w_bf16, pool_scale, x, gate, g, pos0,
          zero_first_halo):
    nb, t, d = x.shape
    tt = min(t, POOL_TILE)
    n_t = t // tt
    if halo_blocks_per_tile is None:
        halo_map = lambda b, i: (b, 0, 0)
    else:
        halo_map = lambda b, i: (b, jnp.maximum(i * halo_blocks_per_tile - 1, 0), 0)
    x_spec = pl.BlockSpec((1, tt, d), lambda b, i: (b, i, 0))
    kern = functools.partial(_pool_kernel, pos0=pos0, zero_first_halo=zero_first_halo)
    return pl.pallas_call(
        kern,
        out_shape=jax.ShapeDtypeStruct(x.shape, F32),
        grid=(nb, n_t),
        in_specs=[x_spec,
                  pl.BlockSpec((1, POOL_HALO, d), halo_map),
                  pl.BlockSpec(w_bf16.shape, lambda b, i: (0, 0, 0)),
                  pl.BlockSpec((1, d), lambda b, i: (0, 0)),
                  x_spec,
                  pl.BlockSpec((1, 1, d), lambda b, i: (b, 0, 0)),
                  pl.BlockSpec((1, d), lambda b, i: (0, 0))],
        out_specs=x_spec,
        compiler_params=_cparams(("arbitrary", "arbitrary")),
        name="pool",
    )(h, halo_src, w_bf16, pool_scale.reshape(1, d), x, gate, g.reshape(1, d))


def _pad_cols(a, width):
    return jnp.pad(a, [(0, 0)] * (a.ndim - 1) + [(0, width - a.shape[-1])])


def _split_b_cols(a, d_b, lora):
    lw, la, lg = lora
    r, k, v = a[..., :d_b], a[..., d_b:2 * d_b], a[..., 2 * d_b:3 * d_b]
    o = 3 * d_b
    parts = [_pad_cols(a[..., o:o + lw], LORA_PAD[0]),
             _pad_cols(a[..., o + lw:o + lw + la], LORA_PAD[1]),
             _pad_cols(a[..., o + lw + la:o + lw + la + lg], LORA_PAD[2])]
    return r, k, v, jnp.concatenate(parts, axis=-1)


def _unsplit_b_cols(row, col0, d_b, lora):
    lw, la, lg = lora
    o = col0 + 3 * d_b
    return jnp.concatenate([row[..., col0:o], row[..., o:o + lw],
                            row[..., o + LORA_PAD[0]:o + LORA_PAD[0] + la],
                            row[..., o + LORA_PAD[0] + LORA_PAD[1]:o + LORA_PAD[0] + LORA_PAD[1] + lg]],
                           axis=-1)


def _pad_rows(a, rows):
    return jnp.pad(a, ((0, 0), (0, rows - a.shape[1]), (0, 0)))


def _trunk(x, mods, past, shift0, wkv0, pool0, pos0, W):
    nb, t, d = x.shape
    d_a, d_b = W['d_a'], W['d_b']
    lora = W['lora']
    col_b = 3 * d_a

    g = W['norm_g'][0]
    mod = mods[0]
    p = _proj_in(x, g[0], mod[0], mod[1], W['w_in'])
    if past is None:
        o_a = _attn_prompt(p, W['bias_rows'], nb, t, d_a)
    else:
        cache_k, cache_v, page_table = past
        o_a = _attn_sample(p, cache_k, cache_v, page_table, 0, W['bias_heads_t'], nb, t, d_a)
    s_parts = tuple(s[:, None, :] for s in _split_b_cols(shift0, d_b, lora))
    vecs = _rwkv_pre(p, s_parts, col_b, nb, t, d_b, W)
    t_pad = -(-t // CHUNK) * CHUNK
    if t_pad != t:
        vecs = [_pad_rows(v, t_pad) for v in vecs]
    o_b, s_new = _rwkv_scan(vecs, W['lnx_g'], W['lnx_b'], wkv0)
    o_b = o_b[:, :t].reshape(nb * t, d_b)
    x = _outproj(o_a, o_b, W['w_out'], x, mod[2], g[1])
    x = _mlp(x, g[2], mod[3], mod[4], W['w_mlp1'][0], W['w_mlp2'][0], mod[5], g[3])
    p3 = p.reshape(nb, t, -1)
    k_new = p3[:, :, d_a:2 * d_a].reshape(nb, 1, t, d_a // DH_A, DH_A)
    v_new = p3[:, :, 2 * d_a:3 * d_a].reshape(nb, 1, t, d_a // DH_A, DH_A)
    shift_new = _unsplit_b_cols(p3[:, -1], col_b, d_b, lora)[None]

    g = W['norm_g'][1]
    mod = mods[1]
    h = _modulate_call(x, g[0], mod[0], mod[1])
    if pool0 is None:
        x = _pool(h, h, min(t, POOL_TILE) // POOL_HALO, W['w_pool'], W['pool_scale'], x, mod[2], g[1], pos0, True)
        pool_new = h[:, t - (POOL_HALO - 1):][None]
    else:
        halo = jnp.pad(pool0, ((0, 0), (1, 0), (0, 0)))
        x = _pool(h, halo, None, W['w_pool'], W['pool_scale'], x, mod[2], g[1], pos0, False)
        pool_new = jnp.concatenate([pool0, h], axis=1)[:, -(POOL_HALO - 1):][None]
    x = _mlp(x, g[2], mod[3], mod[4], W['w_mlp1'][1], W['w_mlp2'][1], mod[5], g[3])
    return x, k_new, v_new, s_new[None], shift_new, pool_new


def kernel(x_prompt, x_sample, cache_k, cache_v, page_table, state_wkv, state_shift, state_pool,
           c_prompt, c_sample, w_ada, b_ada, norm_g, w_in, w_out, sb_bias, mu_shift, w0, w_up, a0,
           a_up, g_up, k_k, k_a, r_k, lnx_g, lnx_b, w_pool, pool_scale, w_mlp1, w_mlp2):
    bp, _, d = x_prompt.shape
    bs, t_s, _ = x_sample.shape
    n_layers = w_ada.shape[0]
    assert n_layers == 2 and w_in.shape[0] == 1 and w_pool.shape[0] == 1
    d_b = w0.shape[1]
    d_a = d - d_b
    n_heads_a = d_a // DH_A
    lora = (w_up.shape[1], a_up.shape[1], g_up.shape[1])
    row = lambda a: a.reshape(1, -1)

    c_all = jnp.concatenate([c_prompt, c_sample], axis=0)
    nb_pad = -(-c_all.shape[0] // 8) * 8
    c_all = jnp.pad(c_all, ((0, nb_pad - c_all.shape[0]), (0, 0)))
    mod = _adaln(c_all, w_ada, b_ada)
    mod = mod.reshape(n_layers, nb_pad, 6, d).transpose(0, 2, 1, 3)[:, :, :, None, :]
    mods_p = [[mod[l, j, :bp] for j in range(6)] for l in range(n_layers)]
    mods_s = [[mod[l, j, bp:bp + bs] for j in range(6)] for l in range(n_layers)]

    w_in0 = w_in[0]
    wb_r, wb_k, wb_v, wb_l = _split_b_cols(w_in0[:, 3 * d_a:], d_b, lora)
    w_in_p = jnp.concatenate([w_in0[:, :3 * d_a], wb_r, wb_k, wb_v, wb_l], axis=1).astype(BF16)
    mu_r, mu_k, mu_v, mu_l = _split_b_cols(mu_shift[0], d_b, lora)
    pad_rows = lambda a, n: jnp.pad(a, ((0, n - a.shape[0]), (0, 0)))
    bias = sb_bias[0].astype(F32)
    W = {
        'd_a': d_a, 'd_b': d_b, 'lora': lora,
        'norm_g': norm_g, 'w_in': w_in_p, 'w_out': w_out[0].astype(BF16),
        'bias_rows': jnp.broadcast_to(bias[:, None], (n_heads_a, 256)),
        'bias_heads_t': jnp.broadcast_to(bias[:, None, None], (n_heads_a, t_s, 128)
                                         ).reshape(n_heads_a * t_s, 128),
        'mu_r': row(mu_r), 'mu_k': row(mu_k), 'mu_v': row(mu_v), 'mu_l': row(mu_l),
        'w0': row(w0[0]), 'w_up': pad_rows(w_up[0], LORA_PAD[0]).astype(BF16),
        'a0': row(a0[0]), 'a_up': pad_rows(a_up[0], LORA_PAD[1]).astype(BF16),
        'g_up': pad_rows(g_up[0], LORA_PAD[2]).astype(BF16),
        'k_k': row(k_k[0]), 'k_a': row(k_a[0]), 'r_k': row(r_k[0]),
        'lnx_g': lnx_g[0], 'lnx_b': lnx_b[0],
        'w_pool': w_pool[0].astype(BF16), 'pool_scale': pool_scale[0],
        'w_mlp1': w_mlp1.astype(BF16), 'w_mlp2': w_mlp2.astype(BF16),
    }

    shift0 = jnp.zeros((bp, state_shift.shape[-1]), F32)
    wkv0 = jnp.zeros((bp,) + state_wkv.shape[2:], F32)
    y_p, k_p, v_p, wkv_p, sh_p, pool_p = _trunk(x_prompt, mods_p, None, shift0, wkv0, None, 0, W)

    past_len = page_table.shape[1] * cache_k.shape[2]
    y_s, k_s, v_s, wkv_s, sh_s, pool_s = _trunk(
        x_sample, mods_s, (cache_k, cache_v, page_table), state_shift[0], state_wkv[0],
        state_pool[0], past_len, W)
    return (y_p, y_s, k_p, v_p, k_s, v_s, wkv_p, wkv_s, sh_p, sh_s, pool_p, pool_s)
```

```python
import functools

import jax
import jax.numpy as jnp
import numpy as np
from jax import lax
from jax.experimental import pallas as pl
from jax.experimental.pallas import tpu as pltpu

F32 = jnp.float32
BF16 = jnp.bfloat16

EPS_RMS = 1e-6
EPS_LNX = 64e-5
EPS_KK = 1e-12
POOL_WINDOWS = (2, 4, 8, 16)
POOL_HALO = 16
POOL_TILE = 256
DH_A = 128
DH_B = 64
LORA_PAD = (128, 128, 256)
CHUNK = 64
PAGES_PER_STEP = 8
HEADS_PER_GROUP = 4
VMEM_LIMIT = 56 * 1024 * 1024
ROW_TILE = 1024
PROJ_COL_TILE = 512
FF_TILE = 512


def _cparams(sem):
    return pltpu.CompilerParams(dimension_semantics=sem, vmem_limit_bytes=VMEM_LIMIT)


def _dot(a, b):
    return jnp.dot(a, b, preferred_element_type=F32)


def _dot_nt(a, b):
    return lax.dot_general(a, b, (((1,), (1,)), ((), ())), preferred_element_type=F32)


def _dot_tn(a, b):
    return lax.dot_general(a, b, (((0,), (0,)), ((), ())), preferred_element_type=F32)


def _split2(x):
    hi = x.astype(BF16)
    lo = (x - hi.astype(F32)).astype(BF16)
    return hi, lo


def _split3(x):
    hi = x.astype(BF16)
    r1 = x - hi.astype(F32)
    mid = r1.astype(BF16)
    lo = (r1 - mid.astype(F32)).astype(BF16)
    return hi, mid, lo


def _dot_exact_rhs(x, m, parts):
    pieces = _split2(x) if parts == 2 else _split3(x)
    out = _dot(pieces[0], m)
    for p in pieces[1:]:
        out = out + _dot(p, m)
    return out


def _sigmoid(x):
    return 1.0 / (1.0 + jnp.exp(-x))


def _softplus(x):
    return jnp.maximum(x, 0.0) + jnp.log(1.0 + jnp.exp(-jnp.abs(x)))


def _rms(x, g):
    ms = jnp.mean(x * x, axis=-1, keepdims=True)
    return x * lax.rsqrt(ms + EPS_RMS) * g


def _adaln_kernel(c_ref, w_ref, b_ref, o_ref):
    c = c_ref[...]
    s = c * _sigmoid(c)
    w = w_ref[0]
    s_hi, s_lo = _split2(s)
    w_hi, w_lo = _split2(w)
    o_ref[0] = _dot(s_hi, w_hi) + _dot(s_lo, w_hi) + _dot(s_hi, w_lo) + b_ref[0]


def _adaln(c_all, w_ada, b_ada):
    n_layers, d, n = w_ada.shape
    nb = c_all.shape[0]
    tn = 512
    return pl.pallas_call(
        _adaln_kernel,
        out_shape=jax.ShapeDtypeStruct((n_layers, nb, n), F32),
        grid=(n_layers, n // tn),
        in_specs=[
            pl.BlockSpec((nb, d), lambda l, j: (0, 0)),
            pl.BlockSpec((1, d, tn), lambda l, j: (l, 0, j)),
            pl.BlockSpec((1, 1, tn), lambda l, j: (l, 0, j)),
        ],
        out_specs=pl.BlockSpec((1, nb, tn), lambda l, j: (l, 0, j)),
        compiler_params=_cparams(("arbitrary", "arbitrary")),
        name="adaln",
    )(c_all, w_ada, b_ada.reshape(n_layers, 1, n))


def _row_tiling(nb, t, tt_max):
    if t >= 64:
        return 1, min(t, tt_max)
    return nb, t


def _modulate(x, g, shift, scale):
    return _rms(x, g) * (1.0 + scale) + shift


def _proj_kernel(x_ref, g_ref, sh_ref, sc_ref, w_ref, o_ref, h_scr):
    @pl.when(pl.program_id(1) == 0)
    def _():
        h = _modulate(x_ref[...], g_ref[...], sh_ref[...], sc_ref[...])
        h_scr[...] = h.reshape(h_scr.shape).astype(BF16)

    o_ref[...] = _dot(h_scr[...], w_ref[0])


def _col_tiles(w, tn):
    d, n = w.shape
    return w.reshape(d, n // tn, tn).transpose(1, 0, 2)


def _proj_in(x, g, shift, scale, w_tiles):
    nb, t, d = x.shape
    n_tiles, _, tn = w_tiles.shape
    n = n_tiles * tn
    nbk, tt = _row_tiling(nb, t, ROW_TILE)
    n_t = t // tt
    rows = nbk * tt
    return pl.pallas_call(
        _proj_kernel,
        out_shape=jax.ShapeDtypeStruct((nb * t, n), F32),
        grid=((nb // nbk) * n_t, n // tn),
        in_specs=[
            pl.BlockSpec((nbk, tt, d), lambda i, j: (i // n_t, i % n_t, 0)),
            pl.BlockSpec((1, d), lambda i, j: (0, 0)),
            pl.BlockSpec((nbk, 1, d), lambda i, j: (i // n_t, 0, 0)),
            pl.BlockSpec((nbk, 1, d), lambda i, j: (i // n_t, 0, 0)),
            pl.BlockSpec((1, d, tn), lambda i, j: (j, 0, 0)),
        ],
        out_specs=pl.BlockSpec((rows, tn), lambda i, j: (i, j)),
        scratch_shapes=[pltpu.VMEM((rows, d), BF16)],
        compiler_params=_cparams(("arbitrary", "arbitrary")),
        name="proj_in",
    )(x, g.reshape(1, d), shift, scale, w_tiles)


def _suffix_ones(n):
    j = lax.broadcasted_iota(jnp.int32, (n, n), 0)
    s = lax.broadcasted_iota(jnp.int32, (n, n), 1)
    m = jnp.where(j > s, 1.0, 0.0).astype(BF16)
    return jnp.concatenate([m, m], axis=0)


def _sb_tile(z, carry, mask, uu):
    sp = _softplus(z)
    neg = -sp
    if mask is not None:
        neg = jnp.where(mask, neg, 0.0)
    hi, lo = _split2(neg)
    local = _dot(jnp.concatenate([hi, lo], axis=1), uu)
    reps = z.shape[1] // carry.shape[1]
    after = local + (carry if reps == 1 else jnp.concatenate([carry] * reps, axis=1))
    w = jnp.exp(z - sp + after)
    if mask is not None:
        w = jnp.where(mask, w, 0.0)
    carry = carry + jnp.sum(neg, axis=-1, keepdims=True)
    return w, carry


def _attn_prompt_kernel(qi_ref, kb_ref, q_ref, k_ref, v_ref, bias_ref, o_ref,
                        acc_scr, carry_scr, *, tq, sub, n_heads, scale):
    s = pl.program_id(1)
    qi = qi_ref[s]
    kb = kb_ref[s]
    n_sub = tq // sub
    uu = _suffix_ones(sub)
    lower = (lax.broadcasted_iota(jnp.int32, (sub, sub), 1)
             < lax.broadcasted_iota(jnp.int32, (sub, sub), 0))

    def sub_tile(h, qs, ks, masked):
        cols = slice(h * DH_A, (h + 1) * DH_A)
        qr = slice(0, tq) if qs is None else slice(qs * sub, (qs + 1) * sub)
        kr = slice(ks * sub, (ks + 1) * sub)
        q = (q_ref[qr, cols] * scale).astype(BF16)
        z = _dot_nt(q, k_ref[kr, cols].astype(BF16)) + bias_ref[h:h + 1, :sub]
        w, carry = _sb_tile(z, carry_scr[h, qr], lower if masked else None, uu)
        carry_scr[h, qr] = carry
        acc_scr[qr, cols] += _dot(w.astype(BF16), v_ref[kr, cols].astype(BF16))

    @pl.when(kb == qi)
    def _():
        acc_scr[...] = jnp.zeros_like(acc_scr)
        carry_scr[...] = jnp.zeros_like(carry_scr)
        for h in range(n_heads):
            for qs in range(n_sub):
                for ks in range(qs, -1, -1):
                    sub_tile(h, qs, ks, ks == qs)

    @pl.when(kb < qi)
    def _():
        for h in range(n_heads):
            for ks in range(n_sub - 1, -1, -1):
                sub_tile(h, None, ks, False)

    @pl.when(kb == 0)
    def _():
        o_ref[...] = acc_scr[...].astype(o_ref.dtype)


def _attn_prompt(p, bias_rows, nb, t, d_a):
    n_heads = d_a // DH_A
    tq = tk = min(512, t)
    sub = min(256, tk)
    nq = t // tq
    pairs = [(qi, kb) for qi in range(nq) for kb in range(qi, -1, -1)]
    qi_tab = jnp.asarray(np.array([a for a, _ in pairs], np.int32))
    kb_tab = jnp.asarray(np.array([b for _, b in pairs], np.int32))
    kern = functools.partial(_attn_prompt_kernel, tq=tq, sub=sub, n_heads=n_heads,
                             scale=DH_A ** -0.5)
    grid_spec = pltpu.PrefetchScalarGridSpec(
        num_scalar_prefetch=2,
        grid=(nb, len(pairs)),
        in_specs=[
            pl.BlockSpec((tq, d_a), lambda b, s, qt, kt: (b * nq + qt[s], 0)),
            pl.BlockSpec((tk, d_a), lambda b, s, qt, kt: (b * nq + kt[s], 1)),
            pl.BlockSpec((tk, d_a), lambda b, s, qt, kt: (b * nq + kt[s], 2)),
            pl.BlockSpec(bias_rows.shape, lambda b, s, qt, kt: (0, 0)),
        ],
        out_specs=pl.BlockSpec((tq, d_a), lambda b, s, qt, kt: (b * nq + qt[s], 0)),
        scratch_shapes=[pltpu.VMEM((tq, d_a), F32), pltpu.VMEM((n_heads, tq, 128), F32)],
    )
    return pl.pallas_call(
        kern,
        out_shape=jax.ShapeDtypeStruct((nb * t, d_a), BF16),
        grid_spec=grid_spec,
        compiler_params=_cparams(("arbitrary", "arbitrary")),
        name="attn_prompt",
    )(qi_tab, kb_tab, p, p, p, bias_rows)


def _attn_sample_kernel(pt_ref, q_ref, kc_ref, vc_ref, bias_ref, *rest,
                        n_heads, t, page, scale):
    n_pg = PAGES_PER_STEP
    k_refs = rest[:n_pg]
    v_refs = rest[n_pg:2 * n_pg]
    o_ref = rest[2 * n_pg]
    acc_scr, carry_scr, qx_scr, kpad_scr, vpad_scr = rest[2 * n_pg + 1:]
    s = pl.program_id(1)
    rows = n_heads * t
    uu = _suffix_ones(page)

    def block(kcat, vcat, mask):
        z = _dot_nt(qx_scr[...].astype(BF16), kcat) + bias_ref[...]
        w, carry = _sb_tile(z, carry_scr[...], mask, uu)
        carry_scr[...] = carry
        acc_scr[...] += _dot(w.astype(BF16), vcat)

    def page_cat(ref):
        return jnp.concatenate([ref[0, 0, pl.ds(h, page, stride=n_heads), :]
                                for h in range(n_heads)], axis=1).astype(BF16)

    @pl.when(s == 0)
    def _():
        acc_scr[...] = jnp.zeros_like(acc_scr)
        carry_scr[...] = jnp.zeros_like(carry_scr)
        qx_scr[...] = jnp.zeros_like(qx_scr)
        kpad_scr[...] = jnp.zeros_like(kpad_scr)
        vpad_scr[...] = jnp.zeros_like(vpad_scr)
        for h in range(n_heads):
            cols = slice(h * DH_A, (h + 1) * DH_A)
            qx_scr[h * t:(h + 1) * t, cols] = q_ref[:, cols] * scale
        kpad_scr[0:t, :] = kc_ref[...]
        vpad_scr[0:t, :] = vc_ref[...]
        q_tok = lax.broadcasted_iota(jnp.int32, (rows, page), 0) % t
        key = lax.broadcasted_iota(jnp.int32, (rows, page), 1)
        block(kpad_scr[...].astype(BF16), vpad_scr[...].astype(BF16), key < q_tok)

    @pl.when(s > 0)
    def _():
        for j in range(n_pg):
            block(page_cat(k_refs[j]), page_cat(v_refs[j]), None)

    @pl.when(s == pl.num_programs(1) - 1)
    def _():
        for h in range(n_heads):
            cols = slice(h * DH_A, (h + 1) * DH_A)
            o_ref[:, cols] = acc_scr[h * t:(h + 1) * t, cols].astype(o_ref.dtype)


def _attn_sample(p, cache_k, cache_v, page_table, layer, bias_rows, nb, t, d_a):
    n_heads = d_a // DH_A
    n_pool, n_ab, page, _, _ = cache_k.shape
    n_pages = page_table.shape[1]
    n_pg = PAGES_PER_STEP
    assert n_pages % n_pg == 0 and page == 128 and t == 8
    ck = cache_k.reshape(n_pool, n_ab, page * n_heads, DH_A)
    cv = cache_v.reshape(n_pool, n_ab, page * n_heads, DH_A)
    n_steps = 1 + n_pages // n_pg
    rows = n_heads * t

    def page_map(j):
        def index_map(b, s, pt):
            idx = n_pages - 1 - (jnp.maximum(s, 1) - 1) * n_pg - j
            return (pt[b, idx], layer, 0, 0)
        return index_map

    cache_specs = [pl.BlockSpec((1, 1, page * n_heads, DH_A), page_map(j)) for j in range(n_pg)]
    kern = functools.partial(_attn_sample_kernel, n_heads=n_heads, t=t, page=page,
                             scale=DH_A ** -0.5)
    grid_spec = pltpu.PrefetchScalarGridSpec(
        num_scalar_prefetch=1,
        grid=(nb, n_steps),
        in_specs=[
            pl.BlockSpec((t, d_a), lambda b, s, pt: (b, 0)),
            pl.BlockSpec((t, d_a), lambda b, s, pt: (b, 1)),
            pl.BlockSpec((t, d_a), lambda b, s, pt: (b, 2)),
            pl.BlockSpec(bias_rows.shape, lambda b, s, pt: (0, 0)),
        ] + cache_specs + cache_specs,
        out_specs=pl.BlockSpec((t, d_a), lambda b, s, pt: (b, 0)),
        scratch_shapes=[pltpu.VMEM((rows, d_a), F32), pltpu.VMEM((rows, 128), F32),
                        pltpu.VMEM((rows, d_a), F32),
                        pltpu.VMEM((page, d_a), F32), pltpu.VMEM((page, d_a), F32)],
    )
    return pl.pallas_call(
        kern,
        out_shape=jax.ShapeDtypeStruct((nb * t, d_a), BF16),
        grid_spec=grid_spec,
        compiler_params=_cparams(("arbitrary", "arbitrary")),
        name="attn_sample",
    )(page_table, p, p, p, bias_rows, *([ck] * n_pg), *([cv] * n_pg))


def _head_ones(n, width):
    i = lax.broadcasted_iota(jnp.int32, (n, n), 0) // width
    j = lax.broadcasted_iota(jnp.int32, (n, n), 1) // width
    return jnp.where(i == j, 1.0, 0.0).astype(BF16)


def _rwkv_pre_kernel(zr_ref, zk_ref, zv_ref, zl_ref, sr_ref, sk_ref, sv_ref, sl_ref,
                     mr_ref, mk_ref, mv_ref, ml_ref, w0_ref, wup_ref, a0_ref, aup_ref, gup_ref,
                     kk_ref, ka_ref, rk_ref,
                     r_out, lw_out, k_out, v_out, a_out, b_out, g_out, bonus_out,
                     prev_r, prev_k, prev_v, prev_l):
    first = pl.program_id(1) == 0

    def shifted(x_ref, init_ref, prev_scr, mu_ref):
        @pl.when(first)
        def _():
            prev_scr[...] = init_ref[0]

        x = x_ref[...]
        n = x.shape[0]
        prev = jnp.where(lax.broadcasted_iota(jnp.int32, x.shape, 0) == 0, prev_scr[...],
                         pltpu.roll(x, 1, 0))
        prev_scr[...] = x[n - 1:n, :]
        return x + mu_ref[...] * (prev - x)

    zr = shifted(zr_ref, sr_ref, prev_r, mr_ref)
    zk = shifted(zk_ref, sk_ref, prev_k, mk_ref)
    zv = shifted(zv_ref, sv_ref, prev_v, mv_ref)
    zl = shifted(zl_ref, sl_ref, prev_l, ml_ref)
    pw, pa, pg = LORA_PAD
    zw, za, zg = zl[:, :pw], zl[:, pw:pw + pa], zl[:, pw + pa:pw + pa + pg]

    lw = w0_ref[...] + _dot(jnp.tanh(zw).astype(BF16), wup_ref[...])
    w_log = -_softplus(-lw) - 0.5
    a_gate = _sigmoid(a0_ref[...] + _dot(za.astype(BF16), aup_ref[...]))
    gate = _dot(_sigmoid(zg).astype(BF16), gup_ref[...])

    ones = _head_ones(zk.shape[1], DH_B)
    kk = zk * kk_ref[...]
    kk = kk * lax.rsqrt(_dot_exact_rhs(kk * kk, ones, 2) + EPS_KK)
    k = zk * (1.0 + (a_gate - 1.0) * ka_ref[...])
    bonus = _dot_exact_rhs(zr * k * rk_ref[...], ones, 2) * zv

    r_out[0] = zr
    lw_out[0] = -jnp.exp(w_log)
    k_out[0] = k
    v_out[0] = zv
    a_out[0] = -kk
    b_out[0] = kk * a_gate
    g_out[0] = gate
    bonus_out[0] = bonus


def _rwkv_pre(p, shift0, col0, nb, t, d_b, wts):
    tt = min(t, 256)
    n_t = t // tt
    wl = sum(LORA_PAD)
    assert col0 % d_b == 0 and (col0 + 3 * d_b) % wl == 0
    cb = col0 // d_b
    cl = (col0 + 3 * d_b) // wl
    s_r, s_k, s_v, s_l = shift0
    row = lambda b, i: b * n_t + i
    full = lambda a: pl.BlockSpec(a.shape, lambda b, i: (0,) * a.ndim)
    vec_args = [wts['mu_r'], wts['mu_k'], wts['mu_v'], wts['mu_l'], wts['w0'], wts['w_up'],
                wts['a0'], wts['a_up'], wts['g_up'], wts['k_k'], wts['k_a'], wts['r_k']]
    out = jax.ShapeDtypeStruct((nb, t, d_b), F32)
    out_spec = pl.BlockSpec((1, tt, d_b), lambda b, i: (b, i, 0))
    return pl.pallas_call(
        _rwkv_pre_kernel,
        out_shape=[out] * 8,
        grid=(nb, n_t),
        in_specs=[
            pl.BlockSpec((tt, d_b), lambda b, i: (row(b, i), cb)),
            pl.BlockSpec((tt, d_b), lambda b, i: (row(b, i), cb + 1)),
            pl.BlockSpec((tt, d_b), lambda b, i: (row(b, i), cb + 2)),
            pl.BlockSpec((tt, wl), lambda b, i: (row(b, i), cl)),
            pl.BlockSpec((1, 1, d_b), lambda b, i: (b, 0, 0)),
            pl.BlockSpec((1, 1, d_b), lambda b, i: (b, 0, 0)),
            pl.BlockSpec((1, 1, d_b), lambda b, i: (b, 0, 0)),
            pl.BlockSpec((1, 1, wl), lambda b, i: (b, 0, 0)),
        ] + [full(a) for a in vec_args],
        out_specs=[out_spec] * 8,
        scratch_shapes=[pltpu.VMEM((1, d_b), F32)] * 3 + [pltpu.VMEM((1, wl), F32)],
        compiler_params=_cparams(("arbitrary", "arbitrary")),
        name="rwkv_pre",
    )(p, p, p, p, s_r, s_k, s_v, s_l, *vec_args)


def _rwkv_scan_kernel(r_ref, lw_ref, k_ref, v_ref, a_ref, b_ref, g_ref, bonus_ref,
                      lng_ref, lnb_ref, s0_ref, o_ref, s_out, s_scr, y_scr, *, n_heads):
    c = CHUNK
    hg = HEADS_PER_GROUP
    gw = hg * DH_B
    gr = hg * c
    assert c == DH_B and n_heads % hg == 0

    @pl.when(pl.program_id(1) == 0)
    def _():
        s_scr[...] = s0_ref[0]

    ti = lax.broadcasted_iota(jnp.int32, (c, c), 0)
    tj = lax.broadcasted_iota(jnp.int32, (c, c), 1)
    incl_f = jnp.where(ti >= tj, 1.0, 0.0).astype(BF16)
    ri = lax.broadcasted_iota(jnp.int32, (gr, gw), 0)
    rj = lax.broadcasted_iota(jnp.int32, (gr, gw), 1)
    same_head = (ri // c) == (rj // c)
    lag = jnp.where(same_head, (ri % c) - (rj % c), -1)
    strict4 = lag > 0
    incl4 = lag >= 0
    eye4 = jnp.where(ri == rj, 1.0, 0.0)

    def rep(x):
        return jnp.concatenate([x] * hg, axis=0)

    def own_head(x):
        return jnp.where(same_head, rep(x), 0.0)

    lw = lw_ref[0]
    hi, mid, lo = _split3(lw)
    cum = _dot(incl_f, hi) + _dot(incl_f, mid) + _dot(incl_f, lo)
    cum_end = cum[c - 1:c, :]
    r, k, v, a, b = r_ref[0], k_ref[0], v_ref[0], a_ref[0], b_ref[0]
    a_t = a * jnp.exp(cum - lw)
    r_t = r * jnp.exp(cum)
    e_neg = jnp.exp(-cum)
    b_t = b * e_neg
    k_t = k * e_neg
    e_end = jnp.exp(cum_end - cum)
    b_e = b * e_end
    k_e = k * e_end
    g_end = jnp.exp(cum_end)

    for g in range(n_heads // hg):
        lanes = slice(g * gw, (g + 1) * gw)
        s_prev = s_scr[:, lanes]
        ar = jnp.concatenate([own_head(a_t[:, lanes]), own_head(r_t[:, lanes])],
                             axis=0).astype(BF16)
        bk = jnp.concatenate([rep(b_t[:, lanes]), rep(k_t[:, lanes])], axis=0).astype(BF16)
        m = _dot_nt(ar, bk)
        lab = jnp.where(strict4, m[:gr, :gr], 0.0)
        lak = jnp.where(strict4, m[:gr, gr:], 0.0)
        prb = jnp.where(incl4, m[gr:, :gr], 0.0)
        prk = jnp.where(incl4, m[gr:, gr:], 0.0)
        ars = _dot_nt(ar, s_prev.astype(BF16))
        v4 = jnp.concatenate([v[:, (g * hg + h) * DH_B:(g * hg + h + 1) * DH_B]
                              for h in range(hg)], axis=0)
        inv = eye4 + lab
        pw = lab
        n = 1
        while 2 * n < c:
            pw_b = pw.astype(BF16)
            pw = _dot(pw_b, pw_b)
            inv = inv + _dot(inv.astype(BF16), pw.astype(BF16))
            n *= 2
        rhs = ars[:gr] + _dot(lak.astype(BF16), v4.astype(BF16))
        u = _dot(inv.astype(BF16), rhs.astype(BF16))
        uv = jnp.concatenate([u, v4], axis=0).astype(BF16)
        y = ars[gr:] + _dot(jnp.concatenate([prb, prk], axis=1).astype(BF16), uv)
        bke = jnp.concatenate([own_head(b_e[:, lanes]), own_head(k_e[:, lanes])],
                              axis=0).astype(BF16)
        s_scr[:, lanes] = s_prev * g_end[:, lanes] + _dot_tn(uv, bke)
        mean = jnp.mean(y, axis=-1, keepdims=True)
        yc = y - mean
        var = jnp.mean(yc * yc, axis=-1, keepdims=True)
        yn = yc * lax.rsqrt(var + EPS_LNX)
        for h in range(hg):
            y_scr[:, (g * hg + h) * DH_B:(g * hg + h + 1) * DH_B] = yn[h * c:(h + 1) * c]

    o_ref[0] = ((y_scr[...] * lng_ref[...] + lnb_ref[...] + bonus_ref[0]) * g_ref[0]).astype(o_ref.dtype)

    @pl.when(pl.program_id(1) == pl.num_programs(1) - 1)
    def _():
        s_out[0] = s_scr[...]


def _rwkv_scan(vecs, lnx_g, lnx_b, s0):
    nb, t, d_b = vecs[0].shape
    n_heads = d_b // DH_B
    tok = pl.BlockSpec((1, CHUNK, d_b), lambda b, i: (b, i, 0))
    vec = pl.BlockSpec((1, d_b), lambda b, i: (0, 0))
    st = pl.BlockSpec((1, DH_B, d_b), lambda b, i: (b, 0, 0))
    s0_lanes = s0.transpose(0, 2, 1, 3).reshape(nb, DH_B, d_b)
    o_b, s_lanes = pl.pallas_call(
        functools.partial(_rwkv_scan_kernel, n_heads=n_heads),
        out_shape=[jax.ShapeDtypeStruct((nb, t, d_b), BF16),
                   jax.ShapeDtypeStruct((nb, DH_B, d_b), F32)],
        grid=(nb, t // CHUNK),
        in_specs=[tok] * 8 + [vec, vec, st],
        out_specs=[tok, st],
        scratch_shapes=[pltpu.VMEM((DH_B, d_b), F32), pltpu.VMEM((CHUNK, d_b), F32)],
        compiler_params=_cparams(("arbitrary", "arbitrary")),
        name="rwkv_scan",
    )(*vecs, lnx_g.reshape(1, d_b), lnx_b.reshape(1, d_b), s0_lanes)
    return o_b, s_lanes.reshape(nb, DH_B, n_heads, DH_B).transpose(0, 2, 1, 3)


def _residual(x, gate, out, g):
    return x + gate * _rms(out.reshape(x.shape), g)


def _outproj_kernel(oa_ref, ob_ref, w_ref, x_ref, gate_ref, g_ref, o_ref):
    d_a = oa_ref.shape[1]
    out = _dot(oa_ref[...], w_ref[:d_a, :]) + _dot(ob_ref[...], w_ref[d_a:, :])
    o_ref[...] = _residual(x_ref[...], gate_ref[...], out, g_ref[...])


def _outproj(o_a, o_b, w_bf16, x, gate, g):
    nb, t, d = x.shape
    nbk, tt = _row_tiling(nb, t, 512)
    n_t = t // tt
    rows = nbk * tt
    x_spec = pl.BlockSpec((nbk, tt, d), lambda i: (i // n_t, i % n_t, 0))
    return pl.pallas_call(
        _outproj_kernel,
        out_shape=jax.ShapeDtypeStruct(x.shape, F32),
        grid=((nb // nbk) * n_t,),
        in_specs=[
            pl.BlockSpec((rows, o_a.shape[1]), lambda i: (i, 0)),
            pl.BlockSpec((rows, o_b.shape[1]), lambda i: (i, 0)),
            pl.BlockSpec(w_bf16.shape, lambda i: (0, 0)),
            x_spec,
            pl.BlockSpec((nbk, 1, d), lambda i: (i // n_t, 0, 0)),
            pl.BlockSpec((1, d), lambda i: (0, 0)),
        ],
        out_specs=x_spec,
        compiler_params=_cparams(("arbitrary",)),
        name="outproj",
    )(o_a, o_b, w_bf16, x, gate, g.reshape(1, d))


def _mlp_kernel(x_ref, gin_ref, sh_ref, sc_ref, w1_ref, w2_ref, gate_ref, gout_ref, o_ref,
                h_scr):
    f = pl.program_id(1)

    @pl.when(f == 0)
    def _():
        h = _modulate(x_ref[...], gin_ref[...], sh_ref[...], sc_ref[...])
        h_scr[...] = h.reshape(h_scr.shape).astype(BF16)
        o_ref[...] = jnp.zeros_like(o_ref)

    hid = jnp.maximum(_dot(h_scr[...], w1_ref[0]), 0.0)
    o_ref[...] += _dot((hid * hid).astype(BF16), w2_ref[...]).reshape(o_ref.shape)

    @pl.when(f == pl.num_programs(1) - 1)
    def _():
        o_ref[...] = _residual(x_ref[...], gate_ref[...], o_ref[...], gout_ref[...])


def _mlp(x, g_in, shift, scale, w1_tiles, w2_bf16, gate, g_out):
    nb, t, d = x.shape
    n_f, _, tf = w1_tiles.shape
    nbk, tt = _row_tiling(nb, t, ROW_TILE)
    n_t = t // tt
    rows = nbk * tt
    x_spec = pl.BlockSpec((nbk, tt, d), lambda i, f: (i // n_t, i % n_t, 0))
    mod_spec = pl.BlockSpec((nbk, 1, d), lambda i, f: (i // n_t, 0, 0))
    vec_spec = pl.BlockSpec((1, d), lambda i, f: (0, 0))
    return pl.pallas_call(
        _mlp_kernel,
        out_shape=jax.ShapeDtypeStruct(x.shape, F32),
        grid=((nb // nbk) * n_t, n_f),
        in_specs=[x_spec, vec_spec, mod_spec, mod_spec,
                  pl.BlockSpec((1, d, tf), lambda i, f: (f, 0, 0)),
                  pl.BlockSpec((tf, d), lambda i, f: (f, 0)),
                  mod_spec, vec_spec],
        out_specs=x_spec,
        scratch_shapes=[pltpu.VMEM((rows, d), BF16)],
        compiler_params=_cparams(("arbitrary", "arbitrary")),
        name="mlp",
    )(x, g_in.reshape(1, d), shift, scale, w1_tiles, w2_bf16, gate, g_out.reshape(1, d))


def _modulate_kernel(x_ref, g_ref, sh_ref, sc_ref, o_ref):
    o_ref[...] = _modulate(x_ref[...], g_ref[...], sh_ref[...], sc_ref[...])


def _modulate_call(x, g, shift, scale):
    nb, t, d = x.shape
    nbk, tt = _row_tiling(nb, t, 512)
    n_t = t // tt
    x_spec = pl.BlockSpec((nbk, tt, d), lambda i: (i // n_t, i % n_t, 0))
    mod_spec = pl.BlockSpec((nbk, 1, d), lambda i: (i // n_t, 0, 0))
    return pl.pallas_call(
        _modulate_kernel,
        out_shape=jax.ShapeDtypeStruct(x.shape, F32),
        grid=((nb // nbk) * n_t,),
        in_specs=[x_spec, pl.BlockSpec((1, d), lambda i: (0, 0)), mod_spec, mod_spec],
        out_specs=x_spec,
        compiler_params=_cparams(("arbitrary",)),
        name="modulate",
    )(x, g.reshape(1, d), shift, scale)


def _pool_kernel(h_ref, halo_ref, w_ref, ps_ref, x_ref, gate_ref, g_ref, o_ref,
                 *, pos0, zero_first_halo):
    i = pl.program_id(1)
    cur = h_ref[0]
    tt, d = cur.shape
    gw = d // len(POOL_WINDOWS)
    halo = halo_ref[0]
    if zero_first_halo:
        halo = jnp.where(i == 0, 0.0, halo)
    ext = jnp.concatenate([halo, cur], axis=0)
    pos = pos0 + i * tt + lax.broadcasted_iota(jnp.int32, (tt, 1), 0)
    outs = []
    run = ext
    span = 1
    for gi, wlen in enumerate(POOL_WINDOWS):
        while span < wlen:
            run = run + pltpu.roll(run, span, 0)
            span *= 2
        win = run[POOL_HALO:, :gw]
        cnt = jnp.minimum(wlen, pos + 1).astype(F32)
        m = win / cnt - cur[:, gi * gw:(gi + 1) * gw]
        outs.append(_dot(m.astype(BF16), w_ref[gi]))
        run = run[:, gw:]
    y = jnp.concatenate(outs, axis=1) * ps_ref[...]
    o_ref[...] = _residual(x_ref[...], gate_ref[...], y, g_ref[...])


def _pool(h, halo_src, halo_blocks_per_tile, w_bf16, pool_scale, x, gate, g, pos0,
          zero_first_halo):
    nb, t, d = x.shape
    tt = min(t, POOL_TILE)
    n_t = t // tt
    if halo_blocks_per_tile is None:
        halo_map = lambda b, i: (b, 0, 0)
    else:
        halo_map = lambda b, i: (b, jnp.maximum(i * halo_blocks_per_tile - 1, 0), 0)
    x_spec = pl.BlockSpec((1, tt, d), lambda b, i: (b, i, 0))
    kern = functools.partial(_pool_kernel, pos0=pos0, zero_first_halo=zero_first_halo)
    return pl.pallas_call(
        kern,
        out_shape=jax.ShapeDtypeStruct(x.shape, F32),
        grid=(nb, n_t),
        in_specs=[x_spec,
                  pl.BlockSpec((1, POOL_HALO, d), halo_map),
                  pl.BlockSpec(w_bf16.shape, lambda b, i: (0, 0, 0)),
                  pl.BlockSpec((1, d), lambda b, i: (0, 0)),
                  x_spec,
                  pl.BlockSpec((1, 1, d), lambda b, i: (b, 0, 0)),
                  pl.BlockSpec((1, d), lambda b, i: (0, 0))],
        out_specs=x_spec,
        compiler_params=_cparams(("arbitrary", "arbitrary")),
        name="pool",
    )(h, halo_src, w_bf16, pool_scale.reshape(1, d), x, gate, g.reshape(1, d))


def _pad_cols(a, width):
    return jnp.pad(a, [(0, 0)] * (a.ndim - 1) + [(0, width - a.shape[-1])])


def _split_b_cols(a, d_b, lora):
    lw, la, lg = lora
    r, k, v = a[..., :d_b], a[..., d_b:2 * d_b], a[..., 2 * d_b:3 * d_b]
    o = 3 * d_b
    parts = [_pad_cols(a[..., o:o + lw], LORA_PAD[0]),
             _pad_cols(a[..., o + lw:o + lw + la], LORA_PAD[1]),
             _pad_cols(a[..., o + lw + la:o + lw + la + lg], LORA_PAD[2])]
    return r, k, v, jnp.concatenate(parts, axis=-1)


def _unsplit_b_cols(row, col0, d_b, lora):
    lw, la, lg = lora
    o = col0 + 3 * d_b
    return jnp.concatenate([row[..., col0:o], row[..., o:o + lw],
                            row[..., o + LORA_PAD[0]:o + LORA_PAD[0] + la],
                            row[..., o + LORA_PAD[0] + LORA_PAD[1]:o + LORA_PAD[0] + LORA_PAD[1] + lg]],
                           axis=-1)


def _pad_rows(a, rows):
    return jnp.pad(a, ((0, 0), (0, rows - a.shape[1]), (0, 0)))


def _trunk(x, mods, past, shift0, wkv0, pool0, pos0, W):
    nb, t, d = x.shape
    d_a, d_b = W['d_a'], W['d_b']
    lora = W['lora']
    col_b = 3 * d_a

    g = W['norm_g'][0]
    mod = mods[0]
    p = _proj_in(x, g[0], mod[0], mod[1], W['w_in'])
    if past is None:
        o_a = _attn_prompt(p, W['bias_rows'], nb, t, d_a)
    else:
        cache_k, cache_v, page_table = past
        o_a = _attn_sample(p, cache_k, cache_v, page_table, 0, W['bias_heads_t'], nb, t, d_a)
    s_parts = tuple(s[:, None, :] for s in _split_b_cols(shift0, d_b, lora))
    vecs = _rwkv_pre(p, s_parts, col_b, nb, t, d_b, W)
    t_pad = -(-t // CHUNK) * CHUNK
    if t_pad != t:
        vecs = [_pad_rows(v, t_pad) for v in vecs]
    o_b, s_new = _rwkv_scan(vecs, W['lnx_g'], W['lnx_b'], wkv0)
    o_b = o_b[:, :t].reshape(nb * t, d_b)
    x = _outproj(o_a, o_b, W['w_out'], x, mod[2], g[1])
    x = _mlp(x, g[2], mod[3], mod[4], W['w_mlp1'][0], W['w_mlp2'][0], mod[5], g[3])
    p3 = p.reshape(nb, t, -1)
    k_new = p3[:, :, d_a:2 * d_a].reshape(nb, 1, t, d_a // DH_A, DH_A)
    v_new = p3[:, :, 2 * d_a:3 * d_a].reshape(nb, 1, t, d_a // DH_A, DH_A)
    shift_new = _unsplit_b_cols(p3[:, -1], col_b, d_b, lora)[None]

    g = W['norm_g'][1]
    mod = mods[1]
    h = _modulate_call(x, g[0], mod[0], mod[1])
    if pool0 is None:
        x = _pool(h, h, min(t, POOL_TILE) // POOL_HALO, W['w_pool'], W['pool_scale'], x, mod[2], g[1], pos0, True)
        pool_new = h[:, t - (POOL_HALO - 1):][None]
    else:
        halo = jnp.pad(pool0, ((0, 0), (1, 0), (0, 0)))
        x = _pool(h, halo, None, W['w_pool'], W['pool_scale'], x, mod[2], g[1], pos0, False)
        pool_new = jnp.concatenate([pool0, h], axis=1)[:, -(POOL_HALO - 1):][None]
    x = _mlp(x, g[2], mod[3], mod[4], W['w_mlp1'][1], W['w_mlp2'][1], mod[5], g[3])
    return x, k_new, v_new, s_new[None], shift_new, pool_new


def kernel(x_prompt, x_sample, cache_k, cache_v, page_table, state_wkv, state_shift, state_pool,
           c_prompt, c_sample, w_ada, b_ada, norm_g, w_in, w_out, sb_bias, mu_shift, w0, w_up, a0,
           a_up, g_up, k_k, k_a, r_k, lnx_g, lnx_b, w_pool, pool_scale, w_mlp1, w_mlp2):
    bp, _, d = x_prompt.shape
    bs, t_s, _ = x_sample.shape
    n_layers = w_ada.shape[0]
    assert n_layers == 2 and w_in.shape[0] == 1 and w_pool.shape[0] == 1
    d_b = w0.shape[1]
    d_a = d - d_b
    n_heads_a = d_a // DH_A
    lora = (w_up.shape[1], a_up.shape[1], g_up.shape[1])
    row = lambda a: a.reshape(1, -1)

    c_all = jnp.concatenate([c_prompt, c_sample], axis=0)
    nb_pad = -(-c_all.shape[0] // 8) * 8
    c_all = jnp.pad(c_all, ((0, nb_pad - c_all.shape[0]), (0, 0)))
    mod = _adaln(c_all, w_ada, b_ada)
    mod = mod.reshape(n_layers, nb_pad, 6, d).transpose(0, 2, 1, 3)[:, :, :, None, :]
    mods_p = [[mod[l, j, :bp] for j in range(6)] for l in range(n_layers)]
    mods_s = [[mod[l, j, bp:bp + bs] for j in range(6)] for l in range(n_layers)]

    w_in0 = w_in[0]
    wb_r, wb_k, wb_v, wb_l = _split_b_cols(w_in0[:, 3 * d_a:], d_b, lora)
    w_in_p = jnp.concatenate([w_in0[:, :3 * d_a], wb_r, wb_k, wb_v, wb_l], axis=1).astype(BF16)
    mu_r, mu_k, mu_v, mu_l = _split_b_cols(mu_shift[0], d_b, lora)
    pad_rows = lambda a, n: jnp.pad(a, ((0, n - a.shape[0]), (0, 0)))
    bias = sb_bias[0].astype(F32)
    W = {
        'd_a': d_a, 'd_b': d_b, 'lora': lora,
        'norm_g': norm_g, 'w_in': _col_tiles(w_in_p, PROJ_COL_TILE),
        'w_out': w_out[0].astype(BF16),
        'bias_rows': jnp.broadcast_to(bias[:, None], (n_heads_a, 256)),
        'bias_heads_t': jnp.broadcast_to(bias[:, None, None], (n_heads_a, t_s, 128)
                                         ).reshape(n_heads_a * t_s, 128),
        'mu_r': row(mu_r), 'mu_k': row(mu_k), 'mu_v': row(mu_v), 'mu_l': row(mu_l),
        'w0': row(w0[0]), 'w_up': pad_rows(w_up[0], LORA_PAD[0]).astype(BF16),
        'a0': row(a0[0]), 'a_up': pad_rows(a_up[0], LORA_PAD[1]).astype(BF16),
        'g_up': pad_rows(g_up[0], LORA_PAD[2]).astype(BF16),
        'k_k': row(k_k[0]), 'k_a': row(k_a[0]), 'r_k': row(r_k[0]),
        'lnx_g': lnx_g[0], 'lnx_b': lnx_b[0],
        'w_pool': w_pool[0].astype(BF16), 'pool_scale': pool_scale[0],
        'w_mlp1': [_col_tiles(w_mlp1[l].astype(BF16), FF_TILE) for l in range(n_layers)],
        'w_mlp2': w_mlp2.astype(BF16),
    }

    shift0 = jnp.zeros((bp, state_shift.shape[-1]), F32)
    wkv0 = jnp.zeros((bp,) + state_wkv.shape[2:], F32)
    y_p, k_p, v_p, wkv_p, sh_p, pool_p = _trunk(x_prompt, mods_p, None, shift0, wkv0, None, 0, W)

    past_len = page_table.shape[1] * cache_k.shape[2]
    y_s, k_s, v_s, wkv_s, sh_s, pool_s = _trunk(
        x_sample, mods_s, (cache_k, cache_v, page_table), state_shift[0], state_wkv[0],
        state_pool[0], past_len, W)
    return (y_p, y_s, k_p, v_p, k_s, v_s, wkv_p, wkv_s, sh_p, sh_s, pool_p, pool_s)
```
